```python
import jax, jax.numpy as jnp
from jax import lax
import numpy as np

D_MODEL = 1024
BATCH = 2
SEQ = 8192
DEPTH = 1
DEC_BATCH = 128
DEC_SEQ = 4
PAST_LEN = 8192
PAGE_SIZE = 128

N_META = 16
CONV_WIDTH = 512
CONV_K = 3
N_HEADS = 8
QK_NOPE = 64
QK_ROPE = 32
QK_HEAD = QK_NOPE + QK_ROPE
V_HEAD = 64
Q_LORA = 384
KV_LORA = 256
ROPE_THETA = 10000.0
ATTN_SCALE = QK_HEAD ** -0.5
NEG_INF = -1e30
Q_BLOCK = 128
PEER_HEADS = 8
PEER_NKEYS = 128
PEER_EXPERTS = PEER_NKEYS * PEER_NKEYS
PEER_QDIM = 256
PEER_HALF = PEER_QDIM // 2
PEER_TOPK = 16
PEER_BLOCK = 256
EPS = 1e-6
IN_WIDTHS = (CONV_WIDTH, CONV_WIDTH, CONV_WIDTH, Q_LORA, KV_LORA, QK_ROPE, D_MODEL, D_MODEL)
IN_TOTAL = sum(IN_WIDTHS)
IN_OFFSETS = np.cumsum(IN_WIDTHS)[:-1].tolist()

kernel_name = 'mla_shortconv_peer_hybrid'


def rmsnorm(x, g):
    xf = x.astype(jnp.float32)
    y = xf * lax.rsqrt(jnp.mean(jnp.square(xf), axis=-1, keepdims=True) + EPS)
    return (y * g.astype(jnp.float32)).astype(x.dtype)


def rope_tables(pos):
    inv_freq = ROPE_THETA ** (-jnp.arange(0, QK_ROPE, 2, dtype=jnp.float32) / QK_ROPE)
    ang = pos.astype(jnp.float32)[:, None] * inv_freq[None, :]
    return jnp.cos(ang), jnp.sin(ang)


def apply_rope(x, cos, sin):
    xf = x.astype(jnp.float32)
    x1, x2 = xf[..., :QK_ROPE // 2], xf[..., QK_ROPE // 2:]
    return jnp.concatenate([x1 * cos - x2 * sin, x2 * cos + x1 * sin], axis=-1).astype(x.dtype)


def in_projection(h, w_in):
    return jnp.split(h @ w_in, IN_OFFSETS, axis=-1)


def short_conv(u_ext, conv_w, n_out):
    return sum(conv_w[k] * u_ext[:, k:k + n_out] for k in range(CONV_K))


def mla_queries(q_lat, cos, sin, g_q_lat, w_uq, g_q_nope, g_q_rope):
    cq = rmsnorm(q_lat, g_q_lat)
    q = jnp.einsum('blc,chd->blhd', cq, w_uq)
    qn = rmsnorm(q[..., :QK_NOPE], g_q_nope)
    qr = apply_rope(rmsnorm(q[..., QK_NOPE:], g_q_rope), cos[:, None, :], sin[:, None, :])
    return qn, qr


def mla_latent(kv_lat, kr_raw, cos, sin, g_kv_lat, g_k_rope):
    return rmsnorm(kv_lat, g_kv_lat), apply_rope(rmsnorm(kr_raw, g_k_rope), cos, sin)


def mla_keys(ckv, w_uk, g_k_nope):
    return rmsnorm(jnp.einsum('...kc,chd->...khd', ckv, w_uk), g_k_nope)


def attend(qn, qr, q_pos, kn, kr, ckv, k_pos, w_uv):
    s = (jnp.einsum('bqhd,bkhd->bhqk', qn, kn, preferred_element_type=jnp.float32)
         + jnp.einsum('bqhd,bkd->bhqk', qr, kr, preferred_element_type=jnp.float32))
    s = jnp.where(k_pos[None, :] <= q_pos[:, None], s * ATTN_SCALE, NEG_INF)
    p = jax.nn.softmax(s, axis=-1).astype(ckv.dtype)
    o_lat = jnp.einsum('bhqk,bkc->bqhc', p, ckv)
    return jnp.einsum('bqhc,chd->bqhd', o_lat, w_uv)


def prompt_attention(qn, qr, kn, kr, ckv, w_uv, with_meta):
    b, n_tot = kn.shape[:2]
    n_real = n_tot - N_META
    nb = n_real // Q_BLOCK
    pos = jnp.arange(n_tot)

    def blocks(a):
        a = a[:, a.shape[1] - n_real:]
        return jnp.moveaxis(a.reshape(b, nb, Q_BLOCK, *a.shape[2:]), 1, 0)

    def one_block(args):
        bqn, bqr, bpos = args
        return attend(bqn, bqr, bpos, kn, kr, ckv, pos, w_uv)

    out = lax.map(one_block, (blocks(qn), blocks(qr), pos[N_META:].reshape(nb, Q_BLOCK)))
    out = jnp.moveaxis(out, 0, 1).reshape(b, n_real, N_HEADS, V_HEAD)
    if with_meta:
        mp = pos[:N_META]
        meta = attend(qn[:, :N_META], qr[:, :N_META], mp, kn[:, :N_META], kr[:, :N_META], ckv[:, :N_META], mp, w_uv)
        out = jnp.concatenate([meta, out], axis=1)
    return out


def sample_attention(qn, qr, ckv_new, kr_new, pool_ckv, pool_kr, page_table, w_uk, w_uv, g_k_nope):
    dec_seq = qn.shape[1]
    q_pos = PAST_LEN + jnp.arange(dec_seq)
    k_pos = jnp.arange(PAST_LEN + dec_seq)

    def one_seq(args):
        pages, sqn, sqr, sckv, skr = args
        ckv_all = jnp.concatenate([pool_ckv[pages].reshape(-1, KV_LORA), sckv], axis=0)
        kr_all = jnp.concatenate([pool_kr[pages].reshape(-1, QK_ROPE), skr], axis=0)
        kn = mla_keys(ckv_all, w_uk, g_k_nope)
        return attend(sqn[None], sqr[None], q_pos, kn[None], kr_all[None], ckv_all[None], k_pos, w_uv)[0]

    return lax.map(one_seq, (page_table, qn, qr, ckv_new, kr_new))


def branch_merge(b_gate, conv_y, attn_heads, gate_c, gate_a, w_conv_out, w_attn_out, w_mix_out):
    conv_o = (b_gate * conv_y) @ w_conv_out
    attn_o = attn_heads.reshape(*attn_heads.shape[:-2], N_HEADS * V_HEAD) @ w_attn_out
    return (jax.nn.sigmoid(gate_c) * conv_o + jax.nn.sigmoid(gate_a) * attn_o) @ w_mix_out


def peer_block(h, w_peer_q, sub_keys, expert_u, expert_v):
    t = h.shape[0]
    q = (h @ w_peer_q).reshape(t, PEER_HEADS, 2, PEER_HALF)
    s = jnp.einsum('thcd,chnd->thcn', q, sub_keys, preferred_element_type=jnp.float32)
    s1, i1 = lax.top_k(s[:, :, 0], PEER_TOPK)
    s2, i2 = lax.top_k(s[:, :, 1], PEER_TOPK)
    cand = (s1[..., :, None] + s2[..., None, :]).reshape(t, PEER_HEADS, PEER_TOPK * PEER_TOPK)
    cidx = (i1[..., :, None] * PEER_NKEYS + i2[..., None, :]).reshape(t, PEER_HEADS, PEER_TOPK * PEER_TOPK)
    best, sel = lax.top_k(cand, PEER_TOPK)
    eidx = jnp.take_along_axis(cidx, sel, axis=-1)
    g = jax.nn.softmax(best, axis=-1).astype(h.dtype)
    a = jax.nn.gelu(jnp.einsum('thkd,td->thk', expert_u[eidx], h), approximate=False)
    return jnp.einsum('thk,thkd->td', g * a, expert_v[eidx])


def peer(h, w_peer_q, sub_keys, expert_u, expert_v):
    b, l, d = h.shape
    n = b * l
    nb = -(-n // PEER_BLOCK)
    flat = jnp.pad(h.reshape(n, d), ((0, nb * PEER_BLOCK - n), (0, 0))).reshape(nb, PEER_BLOCK, d)
    out = lax.map(lambda hb: peer_block(hb, w_peer_q, sub_keys, expert_u, expert_v), flat)
    return out.reshape(nb * PEER_BLOCK, d)[:n].reshape(b, l, d)


def setup_inputs(seed: int = 0) -> dict:
    key = jax.random.key(seed)
    ks = iter(jax.random.split(key, 40))
    nrm = lambda shape, scale: jax.random.normal(next(ks), shape, jnp.float32) * scale
    gain = lambda shape: 1.0 + 0.02 * jax.random.normal(next(ks), shape, jnp.float32)
    n_pages = PAST_LEN // PAGE_SIZE
    n_used = DEC_BATCH * n_pages
    n_phys = n_used + max(1, n_used // 4)
    page_table = jax.random.permutation(next(ks), n_phys)[:n_used].reshape(DEC_BATCH, n_pages).astype(jnp.int32)
    return {
        'x_prompt': nrm((BATCH, SEQ, D_MODEL), 1.0),
        'x_sample': nrm((DEC_BATCH, DEC_SEQ, D_MODEL), 1.0),
        'cache_ckv': nrm((DEPTH, n_phys, PAGE_SIZE, KV_LORA), 1.0),
        'cache_krope': nrm((DEPTH, n_phys, PAGE_SIZE, QK_ROPE), 1.0),
        'state_conv': nrm((DEPTH, DEC_BATCH, CONV_K - 1, CONV_WIDTH), 1.0),
        'page_table': page_table,
        'meta_tokens': nrm((N_META, D_MODEL), 1.0),
        'norm_mix': gain((DEPTH, D_MODEL)),
        'w_in': nrm((DEPTH, D_MODEL, IN_TOTAL), D_MODEL ** -0.5),
        'conv_w': nrm((DEPTH, CONV_K, CONV_WIDTH), CONV_K ** -0.5),
        'w_conv_out': nrm((DEPTH, CONV_WIDTH, D_MODEL), CONV_WIDTH ** -0.5),
        'g_q_lat': gain((DEPTH, Q_LORA)),
        'w_uq': nrm((DEPTH, Q_LORA, N_HEADS, QK_HEAD), Q_LORA ** -0.5),
        'g_q_nope': gain((DEPTH, QK_NOPE)),
        'g_q_rope': gain((DEPTH, QK_ROPE)),
        'g_kv_lat': gain((DEPTH, KV_LORA)),
        'g_k_rope': gain((DEPTH, QK_ROPE)),
        'w_uk': nrm((DEPTH, KV_LORA, N_HEADS, QK_NOPE), KV_LORA ** -0.5),
        'w_uv': nrm((DEPTH, KV_LORA, N_HEADS, V_HEAD), KV_LORA ** -0.5),
        'g_k_nope': gain((DEPTH, QK_NOPE)),
        'w_attn_out': nrm((DEPTH, N_HEADS * V_HEAD, D_MODEL), (N_HEADS * V_HEAD) ** -0.5),
        'w_mix_out': nrm((DEPTH, D_MODEL, D_MODEL), D_MODEL ** -0.5),
        'norm_ffn': gain((DEPTH, D_MODEL)),
        'w_peer_q': nrm((DEPTH, D_MODEL, PEER_HEADS * PEER_QDIM), D_MODEL ** -0.5),
        'peer_sub_keys': nrm((DEPTH, 2, PEER_HEADS, PEER_NKEYS, PEER_HALF), PEER_HALF ** -0.5),
        'peer_u': nrm((DEPTH, PEER_EXPERTS, D_MODEL), D_MODEL ** -0.5),
        'peer_v': nrm((DEPTH, PEER_EXPERTS, D_MODEL), PEER_HEADS ** -0.5),
    }


def reference(x_prompt, x_sample, cache_ckv, cache_krope, state_conv, page_table, meta_tokens, norm_mix, w_in,
              conv_w, w_conv_out, g_q_lat, w_uq, g_q_nope, g_q_rope, g_kv_lat, g_k_rope, w_uk, w_uv, g_k_nope,
              w_attn_out, w_mix_out, norm_ffn, w_peer_q, peer_sub_keys, peer_u, peer_v):
    n_batch, n_real, _ = x_prompt.shape
    n_tot = N_META + n_real
    dec_seq = x_sample.shape[1]
    cos_p, sin_p = rope_tables(jnp.arange(n_tot))
    cos_s, sin_s = rope_tables(PAST_LEN + jnp.arange(dec_seq))
    meta = jnp.broadcast_to(meta_tokens[None].astype(x_prompt.dtype), (n_batch, N_META, D_MODEL))
    xp = jnp.concatenate([meta, x_prompt], axis=1)
    xs = x_sample
    ckv_p, kr_p, conv_p, ckv_s, kr_s, conv_s = [], [], [], [], [], []
    for layer in range(DEPTH):
        r0 = N_META if layer == DEPTH - 1 else 0
        hp = rmsnorm(xp, norm_mix[layer])
        b_g, c_g, x_in, q_lat, kv_lat, kr_raw, gate_c, gate_a = in_projection(hp, w_in[layer])
        u = c_g * x_in
        u_ext = jnp.pad(u, ((0, 0), (CONV_K - 1, 0), (0, 0)))[:, r0:]
        conv_y = short_conv(u_ext, conv_w[layer], n_tot - r0)
        ckv, kr = mla_latent(kv_lat, kr_raw, cos_p, sin_p, g_kv_lat[layer], g_k_rope[layer])
        qn, qr = mla_queries(q_lat[:, r0:], cos_p[r0:], sin_p[r0:], g_q_lat[layer], w_uq[layer],
                             g_q_nope[layer], g_q_rope[layer])
        kn = mla_keys(ckv, w_uk[layer], g_k_nope[layer])
        attn = prompt_attention(qn, qr, kn, kr, ckv, w_uv[layer], r0 == 0)
        mix = branch_merge(b_g[:, r0:], conv_y, attn, gate_c[:, r0:], gate_a[:, r0:],
                           w_conv_out[layer], w_attn_out[layer], w_mix_out[layer])
        xp = xp[:, r0:] + mix
        xp = xp + peer(rmsnorm(xp, norm_ffn[layer]), w_peer_q[layer], peer_sub_keys[layer], peer_u[layer], peer_v[layer])
        ckv_p.append(ckv)
        kr_p.append(kr)
        conv_p.append(u[:, n_tot - (CONV_K - 1):])
        hs = rmsnorm(xs, norm_mix[layer])
        b_g, c_g, x_in, q_lat, kv_lat, kr_raw, gate_c, gate_a = in_projection(hs, w_in[layer])
        u_ext = jnp.concatenate([state_conv[layer], c_g * x_in], axis=1)
        conv_y = short_conv(u_ext, conv_w[layer], dec_seq)
        ckv, kr = mla_latent(kv_lat, kr_raw, cos_s, sin_s, g_kv_lat[layer], g_k_rope[layer])
        qn, qr = mla_queries(q_lat, cos_s, sin_s, g_q_lat[layer], w_uq[layer], g_q_nope[layer], g_q_rope[layer])
        attn = sample_attention(qn, qr, ckv, kr, cache_ckv[layer], cache_krope[layer], page_table,
                                w_uk[layer], w_uv[layer], g_k_nope[layer])
        mix = branch_merge(b_g, conv_y, attn, gate_c, gate_a, w_conv_out[layer], w_attn_out[layer], w_mix_out[layer])
        xs = xs + mix
        xs = xs + peer(rmsnorm(xs, norm_ffn[layer]), w_peer_q[layer], peer_sub_keys[layer], peer_u[layer], peer_v[layer])
        ckv_s.append(ckv)
        kr_s.append(kr)
        conv_s.append(u_ext[:, dec_seq:])
    return (xp, xs, jnp.stack(ckv_p), jnp.stack(kr_p), jnp.stack(conv_p),
            jnp.stack(ckv_s), jnp.stack(kr_s), jnp.stack(conv_s))
```

```python
import functools

import numpy as np
import jax
import jax.numpy as jnp
from jax import lax
from jax.experimental import pallas as pl
from jax.experimental.pallas import tpu as pltpu

D_MODEL = 1024
N_META = 16
CONV_WIDTH = 512
CONV_K = 3
N_HEADS = 8
QK_NOPE = 64
QK_ROPE = 32
QK_HEAD = QK_NOPE + QK_ROPE
V_HEAD = 64
Q_LORA = 384
KV_LORA = 256
ROPE_THETA = 10000.0
ATTN_SCALE = QK_HEAD ** -0.5
NEG_INF = -1e30
PEER_HEADS = 8
PEER_NKEYS = 128
PEER_EXPERTS = PEER_NKEYS * PEER_NKEYS
PEER_HALF = 128
PEER_TOPK = 16
EPS = 1e-6

LANES = 128
HEAD_PAD = 128
HP = N_HEADS * HEAD_PAD
ROPE_LO = QK_NOPE
ROPE_HALF = QK_ROPE // 2

OFF_B, OFF_C, OFF_X = 0, 512, 1024
OFF_QL = 1536
OFF_KV = OFF_QL + Q_LORA
OFF_GC = OFF_KV + KV_LORA
OFF_GA = OFF_GC + D_MODEL
OFF_KR = OFF_GA + D_MODEL
N_IN = OFF_KR + HEAD_PAD

VMEM_LIMIT = 56 * 1024 * 1024

BF = jnp.bfloat16
F32 = jnp.float32
NT_DIMS = (((1,), (1,)), ((), ()))


def _const_spec(shape):
    nd = len(shape)
    return pl.BlockSpec(shape, lambda *_: (0,) * nd, pipeline_mode=pl.Buffered(1))


def _rms(x, g):
    ms = jnp.mean(x * x, axis=-1, keepdims=True)
    return x * lax.rsqrt(ms + EPS) * g


def _group_mean_sq(x, ind):
    sq = x * x
    hi = sq.astype(BF)
    lo = (sq - hi.astype(F32)).astype(BF)
    parts = []
    for p in range(x.shape[1] // 256):
        sl = slice(p * 256, (p + 1) * 256)
        parts.append(jnp.dot(hi[:, sl], ind, preferred_element_type=F32)
                     + jnp.dot(lo[:, sl], ind, preferred_element_type=F32))
    return jnp.concatenate(parts, axis=-1)


def _rope_block(y, c, s1, s2):
    return y * c + pltpu.roll(y, LANES - ROPE_HALF, 1) * s1 + pltpu.roll(y, ROPE_HALF, 1) * s2


def _rope_heads(y, c, s1, s2):
    return jnp.concatenate(
        [_rope_block(y[:, h * HEAD_PAD:(h + 1) * HEAD_PAD], c, s1, s2) for h in range(N_HEADS)], axis=-1)


def _project(x, nmix, win):
    h = _rms(x, nmix).astype(BF)
    return jnp.dot(h, win, preferred_element_type=F32)


def _conv_gate(proj, u, u1, u2, convw, wco):
    conv_y = convw[0:1, :] * u2 + convw[1:2, :] * u1 + convw[2:3, :] * u
    cb = (proj[:, OFF_B:OFF_B + CONV_WIDTH] * conv_y).astype(BF)
    conv_o = jnp.dot(cb, wco, preferred_element_type=F32)
    gc = jax.nn.sigmoid(proj[:, OFF_GC:OFF_GC + D_MODEL]) * conv_o
    sa = jax.nn.sigmoid(proj[:, OFF_GA:OFF_GA + D_MODEL])
    return gc, sa


def _queries(proj, gql, wuq, gq, ind, c, s1, s2):
    cq = _rms(proj[:, OFF_QL:OFF_QL + Q_LORA], gql).astype(BF)
    q = jnp.dot(cq, wuq, preferred_element_type=F32)
    y = q * lax.rsqrt(_group_mean_sq(q, ind) + EPS) * gq
    return _rope_heads(y, c, s1, s2)


def _latent(proj, gkv, gkr, c, s1, s2):
    ckv = _rms(proj[:, OFF_KV:OFF_KV + KV_LORA], gkv)
    x = proj[:, OFF_KR:OFF_KR + HEAD_PAD]
    ms = jnp.sum(x * x, axis=-1, keepdims=True) * (1.0 / QK_ROPE)
    kr = _rope_block(x * lax.rsqrt(ms + EPS) * gkr, c, s1, s2)
    return ckv, kr


def _proj_prompt_kernel(x_ref, init_ref, tc_ref, ts1_ref, ts2_ref, nmix_ref, win_ref, convw_ref, wco_ref,
                        gql_ref, wuq_ref, gq_ref, ind_ref, gkv_ref, gkr_ref, wuk_ref, gk_ref, wuv_ref,
                        gc_ref, sa_ref, q_ref, k_ref, v_ref, ckv_ref, kr_ref, utail_ref, carry_ref):
    @pl.when(pl.program_id(1) == 0)
    def _():
        carry_ref[...] = init_ref[...]

    tm = x_ref.shape[0]
    proj = _project(x_ref[...], nmix_ref[...], win_ref[...])
    c, s1, s2 = tc_ref[...], ts1_ref[...], ts2_ref[...]

    u = proj[:, OFF_C:OFF_C + CONV_WIDTH] * proj[:, OFF_X:OFF_X + CONV_WIDTH]
    row = lax.broadcasted_iota(jnp.int32, (tm, 1), 0)
    p6 = carry_ref[6:7, :]
    p7 = carry_ref[7:8, :]
    u1 = jnp.where(row == 0, p7, pltpu.roll(u, 1, 0))
    u2 = jnp.where(row == 0, p6, jnp.where(row == 1, p7, pltpu.roll(u, 2, 0)))
    carry_ref[...] = u[tm - 8:tm, :]
    utail_ref[...] = u[tm - 8:tm, :]

    gc, sa = _conv_gate(proj, u, u1, u2, convw_ref[...], wco_ref[...])
    gc_ref[...] = gc
    sa_ref[...] = sa

    ind = ind_ref[...]
    q_ref[...] = _queries(proj, gql_ref[...], wuq_ref[...], gq_ref[...], ind, c, s1, s2).astype(BF)

    ckv, kr = _latent(proj, gkv_ref[...], gkr_ref[...], c, s1, s2)
    ckv_ref[...] = ckv
    kr_ref[...] = kr
    ckv_b = ckv.astype(BF)
    kn = jnp.dot(ckv_b, wuk_ref[...], preferred_element_type=F32)
    kn = kn * lax.rsqrt(_group_mean_sq(kn, ind) + EPS) * gk_ref[...]
    k_ref[...] = (kn + jnp.concatenate([kr] * N_HEADS, axis=-1)).astype(BF)
    v_ref[...] = jnp.dot(ckv_b, wuv_ref[...], preferred_element_type=F32).astype(BF)


def _proj_sample_kernel(x_ref, prev1_ref, prev2_ref, tc_ref, ts1_ref, ts2_ref, nmix_ref, win_ref, convw_ref,
                        wco_ref, gql_ref, wuq_ref, gq_ref, ind_ref, gkv_ref, gkr_ref, gk_ref, wabs_ref, selr_ref,
                        gc_ref, sa_ref, qabs_ref, qr_ref, ckv_ref, kr_ref, u_ref, *, dec_seq):
    tm = x_ref.shape[0]
    proj = _project(x_ref[...], nmix_ref[...], win_ref[...])
    c, s1, s2 = tc_ref[...], ts1_ref[...], ts2_ref[...]

    u = proj[:, OFF_C:OFF_C + CONV_WIDTH] * proj[:, OFF_X:OFF_X + CONV_WIDTH]
    u_ref[...] = u
    t = lax.rem(lax.broadcasted_iota(jnp.int32, (tm, 1), 0), dec_seq)
    u1 = jnp.where(t < 1, prev1_ref[...], pltpu.roll(u, 1, 0))
    u2 = jnp.where(t < 2, prev2_ref[...], pltpu.roll(u, 2, 0))
    gc, sa = _conv_gate(proj, u, u1, u2, convw_ref[...], wco_ref[...])
    gc_ref[...] = gc
    sa_ref[...] = sa

    q = _queries(proj, gql_ref[...], wuq_ref[...], gq_ref[...], ind_ref[...], c, s1, s2)
    qr_ref[...] = jnp.dot(q.astype(BF), selr_ref[...], preferred_element_type=F32).astype(BF)
    qg = (q * gk_ref[...]).astype(BF)
    qabs_ref[...] = jnp.concatenate(
        [jnp.dot(qg[:, h * HEAD_PAD:(h + 1) * HEAD_PAD], wabs_ref[h], preferred_element_type=F32)
         for h in range(N_HEADS)], axis=-1).astype(BF)

    ckv, kr = _latent(proj, gkv_ref[...], gkr_ref[...], c, s1, s2)
    ckv_ref[...] = ckv
    kr_ref[...] = kr


def _flash_kernel(q_ref, k_ref, v_ref, km_ref, vm_ref, o_ref, m_ref, l_ref, acc_ref, *, tq):
    qi = pl.program_id(2)
    q = q_ref[...]

    def step(s, v):
        m_prev = m_ref[...]
        m_new = jnp.maximum(m_prev, jnp.max(s, axis=-1, keepdims=True))
        alpha = jnp.exp(m_prev - m_new)
        p = jnp.exp(s - m_new)
        l_ref[...] = alpha * l_ref[...] + jnp.sum(p, axis=-1, keepdims=True)
        acc_ref[...] = alpha * acc_ref[...] + jnp.dot(p.astype(BF), v, preferred_element_type=F32)
        m_ref[...] = m_new

    s0 = lax.dot_general(q, km_ref[...], NT_DIMS, preferred_element_type=F32)
    col0 = lax.broadcasted_iota(jnp.int32, s0.shape, 1)
    s0 = jnp.where(col0 < N_META, s0, NEG_INF)
    m0 = jnp.max(s0, axis=-1, keepdims=True)
    p0 = jnp.exp(s0 - m0)
    m_ref[...] = m0
    l_ref[...] = jnp.sum(p0, axis=-1, keepdims=True)
    acc_ref[...] = jnp.dot(p0.astype(BF), vm_ref[...], preferred_element_type=F32)

    def body(c, carry):
        r0 = pl.multiple_of(c * tq, tq)
        s = lax.dot_general(q, k_ref[pl.ds(r0, tq), :], NT_DIMS, preferred_element_type=F32)
        step(s, v_ref[pl.ds(r0, tq), :])
        return carry

    lax.fori_loop(0, qi, body, 0)

    r0 = pl.multiple_of(qi * tq, tq)
    s = lax.dot_general(q, k_ref[pl.ds(r0, tq), :], NT_DIMS, preferred_element_type=F32)
    row = lax.broadcasted_iota(jnp.int32, s.shape, 0)
    col = lax.broadcasted_iota(jnp.int32, s.shape, 1)
    step(jnp.where(col <= row, s, NEG_INF), v_ref[pl.ds(r0, tq), :])

    o_ref[...] = (acc_ref[...] / l_ref[...]).astype(BF)


def _sample_attn_kernel(pt_ref, qabs_ref, qr_ref, cnew_ref, krnew_ref, wukt_ref, *rest, n_pages, dec_seq):
    del pt_ref
    ckv_refs = rest[:n_pages]
    kr_refs = rest[n_pages:2 * n_pages]
    o_ref = rest[2 * n_pages]
    lhs_ref, m_ref, l_ref, acc_ref, cpad_ref, krpad_ref = rest[2 * n_pages + 1:]
    p = pl.program_id(1)
    n_w = N_HEADS * QK_NOPE
    n_q = dec_seq * N_HEADS

    @pl.when(p == 0)
    def _():
        lhs_ref[0:n_w, :] = wukt_ref[...]
        lhs_ref[n_w:n_w + n_q, :] = qabs_ref[...]
        m_ref[...] = jnp.full(m_ref.shape, NEG_INF, F32)
        l_ref[...] = jnp.zeros(l_ref.shape, F32)
        acc_ref[...] = jnp.zeros(acc_ref.shape, F32)

    def chunk(c_f32, kr_f32, causal):
        c = c_f32.astype(BF)
        big = lax.dot_general(lhs_ref[...], c, NT_DIMS, preferred_element_type=F32)
        ssq = jnp.concatenate(
            [jnp.sum(jnp.square(big[h * QK_NOPE:(h + 1) * QK_NOPE, :]), axis=0, keepdims=True)
             for h in range(N_HEADS)], axis=0)
        r = lax.rsqrt(ssq * (1.0 / QK_NOPE) + EPS)
        sr = lax.dot_general(qr_ref[...], kr_f32.astype(BF), NT_DIMS, preferred_element_type=F32)
        s = big[n_w:n_w + n_q, :] * jnp.concatenate([r] * dec_seq, axis=0) + sr
        if causal:
            row = lax.broadcasted_iota(jnp.int32, s.shape, 0)
            col = lax.broadcasted_iota(jnp.int32, s.shape, 1)
            s = jnp.where(col * N_HEADS <= row, s, NEG_INF)
        m_prev = m_ref[...]
        m_new = jnp.maximum(m_prev, jnp.max(s, axis=-1, keepdims=True))
        alpha = jnp.exp(m_prev - m_new)
        pr = jnp.exp(s - m_new)
        l_ref[...] = alpha * l_ref[...] + jnp.sum(pr, axis=-1, keepdims=True)
        acc_ref[...] = alpha * acc_ref[...] + jnp.dot(pr.astype(BF), c, preferred_element_type=F32)
        m_ref[...] = m_new

    chunk(jnp.concatenate([r_[...] for r_ in ckv_refs], axis=0),
          jnp.concatenate([r_[...] for r_ in kr_refs], axis=0), False)

    @pl.when(p == pl.num_programs(1) - 1)
    def _():
        cpad_ref[...] = jnp.zeros(cpad_ref.shape, F32)
        krpad_ref[...] = jnp.zeros(krpad_ref.shape, F32)
        cpad_ref[0:dec_seq, :] = cnew_ref[...]
        krpad_ref[0:dec_seq, :] = krnew_ref[...]
        chunk(cpad_ref[...], krpad_ref[...], True)
        o_ref[...] = acc_ref[...] / l_ref[...]


def _merge_kernel(x_ref, attn_ref, gc_ref, sa_ref, wuv_ref, wao_ref, wmo_ref, nffn_ref, wpq_ref, keys_ref,
                  x1_ref, h2_ref, st_ref, *, latent):
    if latent:
        ol = attn_ref[...].astype(BF)
        attn = jnp.concatenate(
            [jnp.dot(ol[:, h * KV_LORA:(h + 1) * KV_LORA], wuv_ref[:, h * HEAD_PAD:(h + 1) * HEAD_PAD],
                     preferred_element_type=F32) for h in range(N_HEADS)], axis=-1).astype(BF)
    else:
        attn = attn_ref[...]
    attn_o = jnp.dot(attn, wao_ref[...], preferred_element_type=F32)
    mixin = (gc_ref[...] + sa_ref[...] * attn_o).astype(BF)
    x1 = x_ref[...] + jnp.dot(mixin, wmo_ref[...], preferred_element_type=F32)
    x1_ref[...] = x1
    h2 = _rms(x1, nffn_ref[...]).astype(BF)
    h2_ref[...] = h2
    qp = jnp.dot(h2, wpq_ref[...], preferred_element_type=F32).astype(BF)
    for h in range(PEER_HEADS):
        for c in range(2):
            o = (h * 2 + c) * PEER_HALF
            st_ref[c * PEER_HEADS + h] = lax.dot_general(
                keys_ref[c, h], qp[:, o:o + PEER_HALF], NT_DIMS, preferred_element_type=F32)


def _select_kernel(st_ref, e1_ref, cnt_ref, e2_ref, rk2_ref, rank_ref, vals_ref):
    n_keys, lanes = st_ref.shape[1], st_ref.shape[2]
    key = lax.broadcasted_iota(jnp.int32, (n_keys, lanes), 0).astype(F32)
    big_neg = jnp.float32(-3.0e38)
    hl = (PEER_HEADS, lanes)

    def extract(hc, carry):
        val = st_ref[hc]
        rank = jnp.full((n_keys, lanes), float(PEER_TOPK), F32)
        for r in range(PEER_TOPK):
            m = jnp.max(val, axis=0, keepdims=True)
            idx = jnp.min(jnp.where(val == m, key, float(n_keys)), axis=0, keepdims=True)
            sel = key == idx
            rank = jnp.where(sel, float(r), rank)
            val = jnp.where(sel, big_neg, val)
            vals_ref[hc, r:r + 1, :] = m
        rank_ref[hc] = rank
        return carry

    lax.fori_loop(0, 2 * PEER_HEADS, extract, 0)

    def stacked(c, a):
        return jnp.concatenate([vals_ref[c * PEER_HEADS + h, a:a + 1, :] for h in range(PEER_HEADS)], axis=0)

    v1 = [stacked(0, a) for a in range(PEER_TOPK)]
    v2 = [stacked(1, b) for b in range(PEER_TOPK)] + [jnp.full(hl, big_neg, F32)]
    top = v1[0] + v2[0]

    n = [jnp.zeros(hl, F32) for _ in range(PEER_TOPK)]
    f = [v1[a] + v2[0] for a in range(PEER_TOPK)]
    z = jnp.zeros(hl, F32)
    for k in range(PEER_TOPK):
        best, ba, bn = f[0], jnp.zeros(hl, F32), n[0]
        for a in range(1, min(k, PEER_TOPK - 1) + 1):
            better = f[a] > best
            best = jnp.where(better, f[a], best)
            ba = jnp.where(better, float(a), ba)
            bn = jnp.where(better, n[a], bn)
        z = z + jnp.exp(best - top)
        nb = bn + 1.0
        v2sel = v2[PEER_TOPK]
        for b in range(1, PEER_TOPK):
            v2sel = jnp.where(nb == float(b), v2[b], v2sel)
        for a in range(min(k, PEER_TOPK - 1) + 1):
            upd = ba == float(a)
            n[a] = jnp.where(upd, nb, n[a])
            f[a] = jnp.where(upd, v1[a] + v2sel, f[a])
    inv_z = 1.0 / z

    for h in range(PEER_HEADS):
        rank1 = rank_ref[h]
        cnt = jnp.zeros((n_keys, lanes), F32)
        for a in range(PEER_TOPK):
            cnt = jnp.where(rank1 == float(a), n[a][h:h + 1, :], cnt)
        cnt_ref[h] = cnt
        e1_ref[h] = jnp.exp(st_ref[h] - v1[0][h:h + 1, :]) * inv_z[h:h + 1, :]
        e2_ref[h] = jnp.exp(st_ref[PEER_HEADS + h] - v2[0][h:h + 1, :])
        rk2_ref[h] = rank_ref[PEER_HEADS + h]


def _experts_kernel(h_ref, u_ref, vt_ref, e1_ref, cnt_ref, e2_ref, rk2_ref, x1_ref, y_ref, acc_ref, at_ref, w_ref):
    e = pl.program_id(1)
    te, tt = at_ref.shape
    n_i = te // PEER_NKEYS

    @pl.when(e == 0)
    def _():
        acc_ref[...] = jnp.zeros(acc_ref.shape, F32)

    at_ref[...] = lax.dot_general(u_ref[...], h_ref[...], NT_DIMS, preferred_element_type=F32)

    def body(lt, carry):
        c0 = pl.multiple_of(lt * LANES, LANES)
        for ii in range(n_i):
            a = at_ref[ii * PEER_NKEYS:(ii + 1) * PEER_NKEYS, pl.ds(c0, LANES)]
            gel = 0.5 * a * (1.0 + lax.erf(a * np.float32(2.0 ** -0.5)))
            g = jnp.zeros((PEER_NKEYS, LANES), F32)
            for h in range(PEER_HEADS):
                cnt_row = cnt_ref[h, ii:ii + 1, pl.ds(c0, LANES)]
                e1_row = e1_ref[h, ii:ii + 1, pl.ds(c0, LANES)]
                sel = rk2_ref[h, :, pl.ds(c0, LANES)] < cnt_row
                g = g + jnp.where(sel, e1_row * e2_ref[h, :, pl.ds(c0, LANES)], 0.0)
            w_ref[ii * PEER_NKEYS:(ii + 1) * PEER_NKEYS, pl.ds(c0, LANES)] = (g * gel).astype(BF)
        return carry

    lax.fori_loop(0, tt // LANES, body, 0)

    acc_ref[...] += jnp.dot(vt_ref[...], w_ref[...], preferred_element_type=F32)

    @pl.when(e == pl.num_programs(1) - 1)
    def _():
        y_ref[...] = x1_ref[...] + acc_ref[...].T


def _cparams(sem):
    return pltpu.CompilerParams(dimension_semantics=sem, vmem_limit_bytes=VMEM_LIMIT)


def _rope_tables(pos):
    inv_freq = ROPE_THETA ** (-jnp.arange(0, QK_ROPE, 2, dtype=F32) / QK_ROPE)
    ang = pos.astype(F32)[:, None] * inv_freq[None, :]
    cos, sin = jnp.cos(ang), jnp.sin(ang)
    n = pos.shape[0]
    zeros = lambda w: jnp.zeros((n, w), F32)
    c = jnp.concatenate([jnp.ones((n, ROPE_LO), F32), cos, cos, zeros(HEAD_PAD - QK_HEAD)], axis=1)
    s1 = jnp.concatenate([zeros(ROPE_LO), -sin, zeros(HEAD_PAD - ROPE_LO - ROPE_HALF)], axis=1)
    s2 = jnp.concatenate([zeros(ROPE_LO + ROPE_HALF), sin, zeros(HEAD_PAD - QK_HEAD)], axis=1)
    return c, s1, s2


def _group_matrix():
    lane = np.arange(256)
    head, within = lane // HEAD_PAD, lane % HEAD_PAD
    group = np.where(within < QK_NOPE, 0, np.where(within < QK_HEAD, 1, 2))
    same = (head[:, None] == head[None, :]) & (group[:, None] == group[None, :]) & (group[:, None] < 2)
    size = np.where(group == 0, QK_NOPE, QK_ROPE)
    return jnp.asarray(same / size[None, :], dtype=BF)


def _pad_heads(w, width):
    lead = w.shape[:-2]
    return jnp.pad(w, [(0, 0)] * (len(lead) + 1) + [(0, HEAD_PAD - width)]).reshape(*lead, HP)


def _prep(w_in, conv_w, w_conv_out, g_q_lat, w_uq, g_q_nope, g_q_rope, g_kv_lat, g_k_rope, w_uk, w_uv,
          g_k_nope, w_attn_out, w_mix_out, norm_mix, norm_ffn, w_peer_q, peer_sub_keys):
    kr_cols = jnp.zeros((D_MODEL, HEAD_PAD), F32).at[:, ROPE_LO:QK_HEAD].set(w_in[:, 2176:2208])
    win = jnp.concatenate([w_in[:, :2176], w_in[:, 2208:], kr_cols], axis=1).astype(BF)
    zpad = jnp.zeros((HEAD_PAD - QK_HEAD,), F32)
    gq = jnp.tile(jnp.concatenate([g_q_nope, g_q_rope, zpad]), N_HEADS)[None] * ATTN_SCALE
    gk = jnp.tile(jnp.concatenate([g_k_nope, jnp.zeros((HEAD_PAD - QK_NOPE,), F32)]), N_HEADS)[None]
    gkr = jnp.zeros((1, HEAD_PAD), F32).at[0, ROPE_LO:QK_HEAD].set(g_k_rope)
    wabs = jnp.pad(jnp.transpose(w_uk, (1, 2, 0)), ((0, 0), (0, HEAD_PAD - QK_NOPE), (0, 0))).astype(BF)
    sel = np.zeros((HP, N_HEADS * QK_ROPE), np.float32)
    for h in range(N_HEADS):
        for d in range(QK_ROPE):
            sel[h * HEAD_PAD + ROPE_LO + d, h * QK_ROPE + d] = 1.0
    wao = jnp.pad(w_attn_out.reshape(N_HEADS, V_HEAD, D_MODEL), ((0, 0), (0, HEAD_PAD - V_HEAD), (0, 0)))
    return dict(
        win=win, convw=conv_w, wco=w_conv_out.astype(BF), gql=g_q_lat[None], wuq=_pad_heads(w_uq, QK_HEAD).astype(BF),
        gq=gq, ind=_group_matrix(), gkv=g_kv_lat[None], gkr=gkr, wuk=_pad_heads(w_uk, QK_NOPE).astype(BF), gk=gk,
        wuv=_pad_heads(w_uv, V_HEAD).astype(BF), wabs=wabs, selr=jnp.asarray(sel, dtype=BF),
        wukt=jnp.transpose(w_uk.reshape(KV_LORA, N_HEADS * QK_NOPE)).astype(BF),
        wao=wao.reshape(HP, D_MODEL).astype(BF), wmo=w_mix_out.astype(BF), nmix=norm_mix[None], nffn=norm_ffn[None],
        wpq=w_peer_q.astype(BF), keys=peer_sub_keys.astype(BF))


def _proj_prompt(x, init, tabs, w, tm):
    nb, n, _ = x.shape
    nt = n // tm
    rows = nb * n
    row_spec = lambda width: pl.BlockSpec((tm, width), lambda b, i: (b * nt + i, 0))
    tab_spec = pl.BlockSpec((tm, HEAD_PAD), lambda b, i: (i, 0))
    weights = [w['nmix'], w['win'], w['convw'], w['wco'], w['gql'], w['wuq'], w['gq'], w['ind'], w['gkv'], w['gkr'],
               w['wuk'], w['gk'], w['wuv']]
    return pl.pallas_call(
        _proj_prompt_kernel,
        grid=(nb, nt),
        in_specs=[pl.BlockSpec((None, tm, D_MODEL), lambda b, i: (b, i, 0)), _const_spec(init.shape),
                  tab_spec, tab_spec, tab_spec] + [_const_spec(a.shape) for a in weights],
        out_specs=[row_spec(D_MODEL), row_spec(D_MODEL), row_spec(HP), row_spec(HP), row_spec(HP),
                   row_spec(KV_LORA), row_spec(HEAD_PAD), pl.BlockSpec((None, 8, CONV_WIDTH), lambda b, i: (b, 0, 0))],
        out_shape=[jax.ShapeDtypeStruct((rows, D_MODEL), F32), jax.ShapeDtypeStruct((rows, D_MODEL), F32),
                   jax.ShapeDtypeStruct((rows, HP), BF), jax.ShapeDtypeStruct((rows, HP), BF),
                   jax.ShapeDtypeStruct((rows, HP), BF), jax.ShapeDtypeStruct((rows, KV_LORA), F32),
                   jax.ShapeDtypeStruct((rows, HEAD_PAD), F32), jax.ShapeDtypeStruct((nb, 8, CONV_WIDTH), F32)],
        scratch_shapes=[pltpu.VMEM((8, CONV_WIDTH), F32)],
        compiler_params=_cparams(("arbitrary", "arbitrary")),
        name="proj_prompt",
    )(x, init, *tabs, *weights)


def _proj_sample(x, prev1, prev2, tabs, w, dec_seq):
    rows = x.shape[0]
    weights = [w['nmix'], w['win'], w['convw'], w['wco'], w['gql'], w['wuq'], w['gq'], w['ind'], w['gkv'], w['gkr'],
               w['gk'], w['wabs'], w['selr']]
    ins = [x, prev1, prev2, *tabs, *weights]
    full = lambda shape: pl.BlockSpec(shape, lambda i: (0,) * len(shape))
    outs = [((rows, D_MODEL), F32), ((rows, D_MODEL), F32), ((rows, N_HEADS * KV_LORA), BF),
            ((rows, N_HEADS * QK_ROPE), BF), ((rows, KV_LORA), F32), ((rows, HEAD_PAD), F32), ((rows, CONV_WIDTH), F32)]
    return pl.pallas_call(
        functools.partial(_proj_sample_kernel, dec_seq=dec_seq),
        grid=(1,),
        in_specs=[full(a.shape) for a in ins],
        out_specs=[full(s) for s, _ in outs],
        out_shape=[jax.ShapeDtypeStruct(s, d) for s, d in outs],
        compiler_params=_cparams(("arbitrary",)),
        name="proj_sample",
    )(*ins)


def _flash(q, k, v, km, vm, nb, n, tq):
    nq = n // tq
    return pl.pallas_call(
        functools.partial(_flash_kernel, tq=tq),
        grid=(nb, N_HEADS, nq),
        in_specs=[pl.BlockSpec((tq, HEAD_PAD), lambda b, h, i: (b * nq + i, h)),
                  pl.BlockSpec((n, HEAD_PAD), lambda b, h, i: (b, h)),
                  pl.BlockSpec((n, HEAD_PAD), lambda b, h, i: (b, h)),
                  pl.BlockSpec((HEAD_PAD, HEAD_PAD), lambda b, h, i: (0, h)),
                  pl.BlockSpec((HEAD_PAD, HEAD_PAD), lambda b, h, i: (0, h))],
        out_specs=pl.BlockSpec((tq, HEAD_PAD), lambda b, h, i: (b * nq + i, h)),
        out_shape=jax.ShapeDtypeStruct((nb * n, HP), BF),
        scratch_shapes=[pltpu.VMEM((tq, 1), F32), pltpu.VMEM((tq, 1), F32), pltpu.VMEM((tq, HEAD_PAD), F32)],
        compiler_params=_cparams(("arbitrary", "arbitrary", "arbitrary")),
        name="flash",
    )(q, k, v, km, vm)


def _sample_attn(page_table, qabs, qr, cnew, krnew, wukt, pool_ckv, pool_kr, n_pages):
    nseq, n_q, _ = qabs.shape
    dec_seq = cnew.shape[1]
    page = pool_ckv.shape[1]
    steps = page_table.shape[1] // n_pages
    seq_spec = lambda a: pl.BlockSpec((None,) + a.shape[1:], lambda b, p, pt: (b, 0, 0))
    page_spec = lambda a, j: pl.BlockSpec((None,) + a.shape[1:], lambda b, p, pt: (pt[b, p * n_pages + j], 0, 0))
    return pl.pallas_call(
        functools.partial(_sample_attn_kernel, n_pages=n_pages, dec_seq=dec_seq),
        grid_spec=pltpu.PrefetchScalarGridSpec(
            num_scalar_prefetch=1,
            grid=(nseq, steps),
            in_specs=[seq_spec(qabs), seq_spec(qr), seq_spec(cnew), seq_spec(krnew),
                      pl.BlockSpec(wukt.shape, lambda b, p, pt: (0, 0))]
                     + [page_spec(pool_ckv, j) for j in range(n_pages)]
                     + [page_spec(pool_kr, j) for j in range(n_pages)],
            out_specs=pl.BlockSpec((None, n_q, KV_LORA), lambda b, p, pt: (b, 0, 0)),
            scratch_shapes=[pltpu.VMEM((N_HEADS * QK_NOPE + n_q, KV_LORA), BF), pltpu.VMEM((n_q, 1), F32),
                            pltpu.VMEM((n_q, 1), F32), pltpu.VMEM((n_q, KV_LORA), F32),
                            pltpu.VMEM((page, KV_LORA), F32), pltpu.VMEM((page, QK_ROPE), F32)]),
        out_shape=jax.ShapeDtypeStruct((nseq, n_q, KV_LORA), F32),
        compiler_params=_cparams(("arbitrary", "arbitrary")),
        name="sample_attn",
    )(page_table, qabs, qr, cnew, krnew, wukt, *([pool_ckv] * n_pages), *([pool_kr] * n_pages))


def _merge(x, attn, gc, sa, w, tm, latent):
    rows = x.shape[0]
    row_spec = lambda width: pl.BlockSpec((tm, width), lambda i: (i, 0))
    weights = [w['wuv'], w['wao'], w['wmo'], w['nffn'], w['wpq'], w['keys']]
    return pl.pallas_call(
        functools.partial(_merge_kernel, latent=latent),
        grid=(rows // tm,),
        in_specs=[row_spec(D_MODEL), row_spec(attn.shape[1]), row_spec(D_MODEL), row_spec(D_MODEL)]
                 + [_const_spec(a.shape) for a in weights],
        out_specs=[row_spec(D_MODEL), row_spec(D_MODEL),
                   pl.BlockSpec((2 * PEER_HEADS, PEER_NKEYS, tm), lambda i: (0, 0, i))],
        out_shape=[jax.ShapeDtypeStruct((rows, D_MODEL), F32), jax.ShapeDtypeStruct((rows, D_MODEL), BF),
                   jax.ShapeDtypeStruct((2 * PEER_HEADS, PEER_NKEYS, rows), F32)],
        compiler_params=_cparams(("arbitrary",)),
        name="merge",
    )(x, attn, gc, sa, *weights)


def _select(st):
    rows = st.shape[2]
    head_spec = pl.BlockSpec((PEER_HEADS, PEER_NKEYS, LANES), lambda i: (0, 0, i))
    out = jax.ShapeDtypeStruct((PEER_HEADS, PEER_NKEYS, rows), F32)
    return pl.pallas_call(
        _select_kernel,
        grid=(rows // LANES,),
        in_specs=[pl.BlockSpec((2 * PEER_HEADS, PEER_NKEYS, LANES), lambda i: (0, 0, i))],
        out_specs=[head_spec] * 4,
        out_shape=[out] * 4,
        scratch_shapes=[pltpu.VMEM((2 * PEER_HEADS, PEER_NKEYS, LANES), F32),
                        pltpu.VMEM((2 * PEER_HEADS, PEER_TOPK, LANES), F32)],
        compiler_params=_cparams(("arbitrary",)),
        name="select",
    )(st)


def _experts(h2, x1, e1, cnt, e2, rk2, u_b, vt_b, tt, te):
    rows = h2.shape[0]
    n_i = te // PEER_NKEYS
    tok_spec = pl.BlockSpec((tt, D_MODEL), lambda t, e: (t, 0))
    i_spec = pl.BlockSpec((PEER_HEADS, n_i, tt), lambda t, e: (0, e, t))
    j_spec = pl.BlockSpec((PEER_HEADS, PEER_NKEYS, tt), lambda t, e: (0, 0, t))
    return pl.pallas_call(
        _experts_kernel,
        grid=(rows // tt, PEER_EXPERTS // te),
        in_specs=[tok_spec, pl.BlockSpec((te, D_MODEL), lambda t, e: (e, 0)),
                  pl.BlockSpec((D_MODEL, te), lambda t, e: (0, e)), i_spec, i_spec, j_spec, j_spec, tok_spec],
        out_specs=tok_spec,
        out_shape=jax.ShapeDtypeStruct((rows, D_MODEL), F32),
        scratch_shapes=[pltpu.VMEM((D_MODEL, tt), F32), pltpu.VMEM((te, tt), F32), pltpu.VMEM((te, tt), BF)],
        compiler_params=_cparams(("arbitrary", "arbitrary")),
        name="experts",
    )(h2, u_b, vt_b, e1, cnt, e2, rk2, x1)


def _peer(x, attn, gc, sa, w, u_b, vt_b, tm, tt, te, latent):
    x1, h2, st = _merge(x, attn, gc, sa, w, tm, latent)
    e1, cnt, e2, rk2 = _select(st)
    return _experts(h2, x1, e1, cnt, e2, rk2, u_b, vt_b, tt, te)


def kernel(x_prompt, x_sample, cache_ckv, cache_krope, state_conv, page_table, meta_tokens, norm_mix, w_in, conv_w, w_conv_out, g_q_lat, w_uq, g_q_nope, g_q_rope, g_kv_lat, g_k_rope, w_uk, w_uv, g_k_nope, w_attn_out, w_mix_out, norm_ffn, w_peer_q, peer_sub_keys, peer_u, peer_v):
    depth = w_in.shape[0]
    assert depth == 1, "single-layer step"
    nb, n_real, _ = x_prompt.shape
    nseq, dec_seq, _ = x_sample.shape
    past_len = page_table.shape[1] * cache_ckv.shape[2]
    assert dec_seq >= CONV_K - 1

    w = _prep(w_in[0], conv_w[0], w_conv_out[0], g_q_lat[0], w_uq[0], g_q_nope[0], g_q_rope[0], g_kv_lat[0],
              g_k_rope[0], w_uk[0], w_uv[0], g_k_nope[0], w_attn_out[0], w_mix_out[0], norm_mix[0], norm_ffn[0],
              w_peer_q[0], peer_sub_keys[0])
    u_b = peer_u[0].astype(BF)
    vt_b = jnp.transpose(peer_v[0]).astype(BF)

    tabs_p = _rope_tables(jnp.arange(N_META + n_real))
    tabs_meta = tuple(t[:N_META] for t in tabs_p)
    tabs_real = tuple(t[N_META:] for t in tabs_p)
    tabs_s = tuple(jnp.tile(t, (nseq, 1)) for t in _rope_tables(past_len + jnp.arange(dec_seq)))

    zero_init = jnp.zeros((8, CONV_WIDTH), F32)
    _, _, _, k_m, v_m, ckv_m, kr_m, u_m = _proj_prompt(meta_tokens[None], zero_init, tabs_meta, w, N_META)
    gc, sa, q, k, v, ckv, kr, utail = _proj_prompt(x_prompt, u_m[0], tabs_real, w, 256)
    pad_rows = lambda a: jnp.pad(a, ((0, HEAD_PAD - N_META), (0, 0)))
    attn = _flash(q, k, v, pad_rows(k_m), pad_rows(v_m), nb, n_real, 512)
    y_prompt = _peer(x_prompt.reshape(nb * n_real, D_MODEL), attn, gc, sa, w, u_b, vt_b, 256, 512, 1024, False)

    st = state_conv[0]
    prev2 = jnp.pad(st, ((0, 0), (0, dec_seq - (CONV_K - 1)), (0, 0))).reshape(nseq * dec_seq, CONV_WIDTH)
    prev1 = jnp.pad(st[:, 1:], ((0, 0), (0, dec_seq - 1), (0, 0))).reshape(nseq * dec_seq, CONV_WIDTH)
    xs = x_sample.reshape(nseq * dec_seq, D_MODEL)
    gc_s, sa_s, qabs, qr, ckv_s, kr_s, u_s = _proj_sample(xs, prev1, prev2, tabs_s, w, dec_seq)
    o_lat = _sample_attn(
        page_table, qabs.reshape(nseq, dec_seq * N_HEADS, KV_LORA), qr.reshape(nseq, dec_seq * N_HEADS, QK_ROPE),
        ckv_s.reshape(nseq, dec_seq, KV_LORA), kr_s[:, ROPE_LO:QK_HEAD].reshape(nseq, dec_seq, QK_ROPE),
        w['wukt'], cache_ckv[0], cache_krope[0], 16)
    y_sample = _peer(xs, o_lat.reshape(nseq * dec_seq, N_HEADS * KV_LORA), gc_s, sa_s, w, u_b, vt_b,
                     256, 512, 1024, True)

    bcast = lambda a: jnp.broadcast_to(a[None], (nb,) + a.shape)
    new_ckv_p = jnp.concatenate([bcast(ckv_m), ckv.reshape(nb, n_real, KV_LORA)], axis=1)
    kr_all = jnp.concatenate([bcast(kr_m), kr.reshape(nb, n_real, HEAD_PAD)], axis=1)[:, :, ROPE_LO:QK_HEAD]
    return (y_prompt.reshape(nb, n_real, D_MODEL), y_sample.reshape(nseq, dec_seq, D_MODEL),
            new_ckv_p[None], kr_all[None], utail[:, 8 - (CONV_K - 1):][None],
            ckv_s.reshape(1, nseq, dec_seq, KV_LORA),
            kr_s[:, ROPE_LO:QK_HEAD].reshape(1, nseq, dec_seq, QK_ROPE),
            u_s.reshape(nseq, dec_seq, CONV_WIDTH)[:, dec_seq - (CONV_K - 1):][None])
```

```python
import functools

import numpy as np
import jax
import jax.numpy as jnp
from jax import lax
from jax.experimental import pallas as pl
from jax.experimental.pallas import tpu as pltpu

D_MODEL = 1024
N_META = 16
CONV_WIDTH = 512
CONV_K = 3
N_HEADS = 8
QK_NOPE = 64
QK_ROPE = 32
QK_HEAD = QK_NOPE + QK_ROPE
V_HEAD = 64
Q_LORA = 384
KV_LORA = 256
ROPE_THETA = 10000.0
ATTN_SCALE = QK_HEAD ** -0.5
NEG_INF = -1e30
PEER_HEADS = 8
PEER_NKEYS = 128
PEER_EXPERTS = PEER_NKEYS * PEER_NKEYS
PEER_HALF = 128
PEER_TOPK = 16
TAKEN = 2.0 ** 100
EPS = 1e-6

LANES = 128
HEAD_PAD = 128
HP = N_HEADS * HEAD_PAD
ROPE_LO = QK_NOPE
ROPE_HALF = QK_ROPE // 2
ONES_LANE = V_HEAD
LOG2E = 1.4426950408889634

OFF_B, OFF_C, OFF_X = 0, 512, 1024
OFF_QL = 1536
OFF_KV = OFF_QL + Q_LORA
OFF_GC = OFF_KV + KV_LORA
OFF_GA = OFF_GC + D_MODEL
OFF_KR = OFF_GA + D_MODEL
N_IN = OFF_KR + HEAD_PAD

VMEM_LIMIT = 56 * 1024 * 1024

BF = jnp.bfloat16
F32 = jnp.float32
NT_DIMS = (((1,), (1,)), ((), ()))


def _const_spec(shape):
    nd = len(shape)
    return pl.BlockSpec(shape, lambda *_: (0,) * nd, pipeline_mode=pl.Buffered(1))


def _rms(x, g):
    ms = jnp.mean(x * x, axis=-1, keepdims=True)
    return x * lax.rsqrt(ms + EPS) * g


def _group_mean_sq(x, ind):
    sq = x * x
    hi = sq.astype(BF)
    lo = (sq - hi.astype(F32)).astype(BF)
    parts = []
    for p in range(x.shape[1] // 256):
        sl = slice(p * 256, (p + 1) * 256)
        parts.append(jnp.dot(hi[:, sl], ind, preferred_element_type=F32)
                     + jnp.dot(lo[:, sl], ind, preferred_element_type=F32))
    return jnp.concatenate(parts, axis=-1)


def _rope_block(y, c, s1, s2):
    return y * c + pltpu.roll(y, LANES - ROPE_HALF, 1) * s1 + pltpu.roll(y, ROPE_HALF, 1) * s2


def _rope_heads(y, c, s1, s2):
    return jnp.concatenate(
        [_rope_block(y[:, h * HEAD_PAD:(h + 1) * HEAD_PAD], c, s1, s2) for h in range(N_HEADS)], axis=-1)


def _project(x, nmix, win):
    h = _rms(x, nmix).astype(BF)
    return jnp.dot(h, win, preferred_element_type=F32)


def _conv_gate(proj, u, u1, u2, convw, wco):
    conv_y = convw[0:1, :] * u2 + convw[1:2, :] * u1 + convw[2:3, :] * u
    cb = (proj[:, OFF_B:OFF_B + CONV_WIDTH] * conv_y).astype(BF)
    conv_o = jnp.dot(cb, wco, preferred_element_type=F32)
    gc = jax.nn.sigmoid(proj[:, OFF_GC:OFF_GC + D_MODEL]) * conv_o
    sa = jax.nn.sigmoid(proj[:, OFF_GA:OFF_GA + D_MODEL])
    return gc, sa


def _queries(proj, gql, wuq, gq, ind, c, s1, s2):
    cq = _rms(proj[:, OFF_QL:OFF_QL + Q_LORA], gql).astype(BF)
    q = jnp.dot(cq, wuq, preferred_element_type=F32)
    y = q * lax.rsqrt(_group_mean_sq(q, ind) + EPS) * gq
    return _rope_heads(y, c, s1, s2)


def _latent(proj, gkv, gkr, c, s1, s2):
    ckv = _rms(proj[:, OFF_KV:OFF_KV + KV_LORA], gkv)
    x = proj[:, OFF_KR:OFF_KR + HEAD_PAD]
    ms = jnp.sum(x * x, axis=-1, keepdims=True) * (1.0 / QK_ROPE)
    kr = _rope_block(x * lax.rsqrt(ms + EPS) * gkr, c, s1, s2)
    return ckv, kr


def _proj_prompt_kernel(x_ref, init_ref, tc_ref, ts1_ref, ts2_ref, nmix_ref, win_ref, convw_ref, wco_ref,
                        gql_ref, wuq_ref, gq_ref, ind_ref, gkv_ref, gkr_ref, wuk_ref, gk_ref, wuv_ref, vone_ref,
                        gc_ref, sa_ref, q_ref, k_ref, v_ref, ckv_ref, kr_ref, utail_ref, carry_ref):
    @pl.when(pl.program_id(1) == 0)
    def _():
        carry_ref[...] = init_ref[...]

    tm = x_ref.shape[0]
    proj = _project(x_ref[...], nmix_ref[...], win_ref[...])
    c, s1, s2 = tc_ref[...], ts1_ref[...], ts2_ref[...]

    u = proj[:, OFF_C:OFF_C + CONV_WIDTH] * proj[:, OFF_X:OFF_X + CONV_WIDTH]
    row = lax.broadcasted_iota(jnp.int32, (tm, 1), 0)
    p6 = carry_ref[6:7, :]
    p7 = carry_ref[7:8, :]
    u1 = jnp.where(row == 0, p7, pltpu.roll(u, 1, 0))
    u2 = jnp.where(row == 0, p6, jnp.where(row == 1, p7, pltpu.roll(u, 2, 0)))
    carry_ref[...] = u[tm - 8:tm, :]
    utail_ref[...] = u[tm - 8:tm, :]

    gc, sa = _conv_gate(proj, u, u1, u2, convw_ref[...], wco_ref[...])
    gc_ref[...] = gc
    sa_ref[...] = sa

    ind = ind_ref[...]
    q_ref[...] = _queries(proj, gql_ref[...], wuq_ref[...], gq_ref[...], ind, c, s1, s2).astype(BF)

    ckv, kr = _latent(proj, gkv_ref[...], gkr_ref[...], c, s1, s2)
    ckv_ref[...] = ckv
    kr_ref[...] = kr
    ckv_b = ckv.astype(BF)
    kn = jnp.dot(ckv_b, wuk_ref[...], preferred_element_type=F32)
    kn = kn * lax.rsqrt(_group_mean_sq(kn, ind) + EPS) * gk_ref[...]
    k_ref[...] = (kn + jnp.concatenate([kr] * N_HEADS, axis=-1)).astype(BF)
    v_ref[...] = (jnp.dot(ckv_b, wuv_ref[...], preferred_element_type=F32) + vone_ref[...]).astype(BF)


def _proj_sample_kernel(x_ref, prev1_ref, prev2_ref, tc_ref, ts1_ref, ts2_ref, nmix_ref, win_ref, convw_ref,
                        wco_ref, gql_ref, wuq_ref, gq_ref, ind_ref, gkv_ref, gkr_ref, gk_ref, wabs_ref, selr_ref,
                        gc_ref, sa_ref, qabs_ref, qr_ref, ckv_ref, kr_ref, u_ref, *, dec_seq):
    tm = x_ref.shape[0]
    proj = _project(x_ref[...], nmix_ref[...], win_ref[...])
    c, s1, s2 = tc_ref[...], ts1_ref[...], ts2_ref[...]

    u = proj[:, OFF_C:OFF_C + CONV_WIDTH] * proj[:, OFF_X:OFF_X + CONV_WIDTH]
    u_ref[...] = u
    t = lax.rem(lax.broadcasted_iota(jnp.int32, (tm, 1), 0), dec_seq)
    u1 = jnp.where(t < 1, prev1_ref[...], pltpu.roll(u, 1, 0))
    u2 = jnp.where(t < 2, prev2_ref[...], pltpu.roll(u, 2, 0))
    gc, sa = _conv_gate(proj, u, u1, u2, convw_ref[...], wco_ref[...])
    gc_ref[...] = gc
    sa_ref[...] = sa

    q = _queries(proj, gql_ref[...], wuq_ref[...], gq_ref[...], ind_ref[...], c, s1, s2)
    qr_ref[...] = jnp.dot(q.astype(BF), selr_ref[...], preferred_element_type=F32).astype(BF)
    qg = (q * gk_ref[...]).astype(BF)
    qabs_ref[...] = jnp.concatenate(
        [jnp.dot(qg[:, h * HEAD_PAD:(h + 1) * HEAD_PAD], wabs_ref[h], preferred_element_type=F32)
         for h in range(N_HEADS)], axis=-1).astype(BF)

    ckv, kr = _latent(proj, gkv_ref[...], gkr_ref[...], c, s1, s2)
    ckv_ref[...] = ckv
    kr_ref[...] = kr


def _flash_kernel(q_ref, k_ref, v_ref, km_ref, vm_ref, o_ref, m_ref, acc_ref, *, tq, nh):
    qi = pl.program_id(2)
    hs = [slice(h * HEAD_PAD, (h + 1) * HEAD_PAD) for h in range(nh)]

    def scores(h, k):
        return lax.dot_general(q_ref[:, hs[h]], k, NT_DIMS, preferred_element_type=F32)

    def step(h, s, v):
        m_prev = m_ref[h]
        m_new = jnp.maximum(m_prev, jnp.max(s, axis=-1, keepdims=True))
        p = jnp.exp2(s - pltpu.repeat(m_new, s.shape[1] // LANES, 1))
        acc_ref[h] = jnp.exp2(m_prev - m_new) * acc_ref[h] + jnp.dot(p.astype(BF), v, preferred_element_type=F32)
        m_ref[h] = m_new

    for h in range(nh):
        s0 = scores(h, km_ref[:, hs[h]])
        col0 = lax.broadcasted_iota(jnp.int32, s0.shape, 1)
        s0 = jnp.where(col0 < N_META, s0, NEG_INF)
        m0 = jnp.max(s0, axis=-1, keepdims=True)
        m_ref[h] = jnp.broadcast_to(m0, s0.shape)
        acc_ref[h] = jnp.dot(jnp.exp2(s0 - m0).astype(BF), vm_ref[:, hs[h]], preferred_element_type=F32)

    def body(c, carry):
        r0 = pl.multiple_of(c * tq, tq)
        for h in range(nh):
            step(h, scores(h, k_ref[pl.ds(r0, tq), hs[h]]), v_ref[pl.ds(r0, tq), hs[h]])
        return carry

    lax.fori_loop(0, qi, body, 0)

    r0 = pl.multiple_of(qi * tq, tq)
    row = lax.broadcasted_iota(jnp.int32, (tq, tq), 0)
    col = lax.broadcasted_iota(jnp.int32, (tq, tq), 1)
    for h in range(nh):
        s = scores(h, k_ref[pl.ds(r0, tq), hs[h]])
        step(h, jnp.where(col <= row, s, NEG_INF), v_ref[pl.ds(r0, tq), hs[h]])
        acc = acc_ref[h]
        o_ref[:, hs[h]] = (acc / acc[:, ONES_LANE:ONES_LANE + 1]).astype(BF)


def _sample_attn_kernel(pt_ref, qabs_ref, qr_ref, cnew_ref, krnew_ref, wukt_ref, *rest, n_pages, dec_seq):
    del pt_ref
    ckv_refs = rest[:n_pages]
    kr_refs = rest[n_pages:2 * n_pages]
    o_ref = rest[2 * n_pages]
    lhs_ref, m_ref, l_ref, acc_ref, cpad_ref, krpad_ref = rest[2 * n_pages + 1:]
    p = pl.program_id(1)
    n_w = N_HEADS * QK_NOPE
    n_q = dec_seq * N_HEADS

    @pl.when(p == 0)
    def _():
        lhs_ref[0:n_w, :] = wukt_ref[...]
        lhs_ref[n_w:n_w + n_q, :] = qabs_ref[...]
        m_ref[...] = jnp.full(m_ref.shape, NEG_INF, F32)
        l_ref[...] = jnp.zeros(l_ref.shape, F32)
        acc_ref[...] = jnp.zeros(acc_ref.shape, F32)

    def chunk(c_f32, kr_f32, causal):
        c = c_f32.astype(BF)
        big = lax.dot_general(lhs_ref[...], c, NT_DIMS, preferred_element_type=F32)
        ssq = jnp.concatenate(
            [jnp.sum(jnp.square(big[h * QK_NOPE:(h + 1) * QK_NOPE, :]), axis=0, keepdims=True)
             for h in range(N_HEADS)], axis=0)
        r = lax.rsqrt(ssq * (1.0 / QK_NOPE) + EPS)
        sr = lax.dot_general(qr_ref[...], kr_f32.astype(BF), NT_DIMS, preferred_element_type=F32)
        s = big[n_w:n_w + n_q, :] * jnp.concatenate([r] * dec_seq, axis=0) + sr
        if causal:
            row = lax.broadcasted_iota(jnp.int32, s.shape, 0)
            col = lax.broadcasted_iota(jnp.int32, s.shape, 1)
            s = jnp.where(col * N_HEADS <= row, s, NEG_INF)
        m_prev = m_ref[...]
        m_new = jnp.maximum(m_prev, jnp.max(s, axis=-1, keepdims=True))
        alpha = jnp.exp2(m_prev - m_new)
        pr = jnp.exp2(s - m_new)
        l_ref[...] = alpha * l_ref[...] + jnp.sum(pr, axis=-1, keepdims=True)
        acc_ref[...] = alpha * acc_ref[...] + jnp.dot(pr.astype(BF), c, preferred_element_type=F32)
        m_ref[...] = m_new

    chunk(jnp.concatenate([r_[...] for r_ in ckv_refs], axis=0),
          jnp.concatenate([r_[...] for r_ in kr_refs], axis=0), False)

    @pl.when(p == pl.num_programs(1) - 1)
    def _():
        cpad_ref[...] = jnp.zeros(cpad_ref.shape, F32)
        krpad_ref[...] = jnp.zeros(krpad_ref.shape, F32)
        cpad_ref[0:dec_seq, :] = cnew_ref[...]
        krpad_ref[0:dec_seq, :] = krnew_ref[...]
        chunk(cpad_ref[...], krpad_ref[...], True)
        o_ref[...] = acc_ref[...] / l_ref[...]


def _merge_kernel(x_ref, attn_ref, gc_ref, sa_ref, wuv_ref, wao_ref, wmo_ref, nffn_ref, wpq_ref, keys_ref,
                  x1_ref, h2_ref, st_ref, *, latent):
    if latent:
        ol = attn_ref[...].astype(BF)
        attn = jnp.concatenate(
            [jnp.dot(ol[:, h * KV_LORA:(h + 1) * KV_LORA], wuv_ref[:, h * HEAD_PAD:(h + 1) * HEAD_PAD],
                     preferred_element_type=F32) for h in range(N_HEADS)], axis=-1).astype(BF)
    else:
        attn = attn_ref[...]
    attn_o = jnp.dot(attn, wao_ref[...], preferred_element_type=F32)
    mixin = (gc_ref[...] + sa_ref[...] * attn_o).astype(BF)
    x1 = x_ref[...] + jnp.dot(mixin, wmo_ref[...], preferred_element_type=F32)
    x1_ref[...] = x1
    h2 = _rms(x1, nffn_ref[...]).astype(BF)
    h2_ref[...] = h2
    qp = jnp.dot(h2, wpq_ref[...], preferred_element_type=F32).astype(BF)
    for h in range(PEER_HEADS):
        for c in range(2):
            o = (h * 2 + c) * PEER_HALF
            st_ref[c * PEER_HEADS + h] = lax.dot_general(
                keys_ref[c, h], qp[:, o:o + PEER_HALF], NT_DIMS, preferred_element_type=F32)


def _select_kernel(st_ref, e1_ref, cnt_ref, e2_ref, rk2_ref, rank_ref, vals_ref):
    n_keys, lanes = st_ref.shape[1], st_ref.shape[2]
    key = lax.broadcasted_iota(jnp.int32, (n_keys, lanes), 0).astype(F32)
    big_neg = jnp.float32(-3.0e38)
    hl = (PEER_HEADS, lanes)

    def extract(pair, carry):
        hcs = [2 * pair, 2 * pair + 1]
        val = [st_ref[hc] for hc in hcs]
        for r in range(PEER_TOPK):
            for c in range(2):
                m = jnp.max(val[c], axis=0, keepdims=True)
                idx = jnp.min(jnp.where(val[c] == m, key, float(n_keys)), axis=0, keepdims=True)
                val[c] = jnp.where(key == idx, -TAKEN * (1.0 + r / 32.0), val[c])
                vals_ref[hcs[c], r:r + 1, :] = m
        for c in range(2):
            taken = val[c] <= -TAKEN
            rank_ref[hcs[c]] = jnp.where(taken, (val[c] * (-1.0 / TAKEN) - 1.0) * 32.0, float(PEER_TOPK))
        return carry

    lax.fori_loop(0, PEER_HEADS, extract, 0)

    def stacked(c, a):
        return jnp.concatenate([vals_ref[c * PEER_HEADS + h, a:a + 1, :] for h in range(PEER_HEADS)], axis=0)

    v1 = [stacked(0, a) for a in range(PEER_TOPK)]
    v2 = [stacked(1, b) for b in range(PEER_TOPK)] + [jnp.full(hl, big_neg, F32)]
    top = v1[0] + v2[0]

    n = [jnp.zeros(hl, F32) for _ in range(PEER_TOPK)]
    f = [v1[a] + v2[0] for a in range(PEER_TOPK)]
    z = jnp.zeros(hl, F32)
    for k in range(PEER_TOPK):
        best, ba, bn = f[0], jnp.zeros(hl, F32), n[0]
        for a in range(1, min(k, PEER_TOPK - 1) + 1):
            better = f[a] > best
            best = jnp.where(better, f[a], best)
            ba = jnp.where(better, float(a), ba)
            bn = jnp.where(better, n[a], bn)
        z = z + jnp.exp(best - top)
        nb = bn + 1.0
        v2sel = v2[PEER_TOPK]
        for b in range(1, PEER_TOPK):
            v2sel = jnp.where(nb == float(b), v2[b], v2sel)
        for a in range(min(k, PEER_TOPK - 1) + 1):
            upd = ba == float(a)
            n[a] = jnp.where(upd, nb, n[a])
            f[a] = jnp.where(upd, v1[a] + v2sel, f[a])
    inv_z = 1.0 / z

    for h in range(PEER_HEADS):
        rank1 = rank_ref[h]
        cnt = jnp.zeros((n_keys, lanes), F32)
        for a in range(PEER_TOPK):
            cnt = jnp.where(rank1 == float(a), n[a][h:h + 1, :], cnt)
        cnt_ref[h] = cnt
        e1_ref[h] = jnp.exp(st_ref[h] - v1[0][h:h + 1, :]) * inv_z[h:h + 1, :]
        e2_ref[h] = jnp.exp(st_ref[PEER_HEADS + h] - v2[0][h:h + 1, :])
        rk2_ref[h] = rank_ref[PEER_HEADS + h]


def _experts_kernel(h_ref, u0_ref, u_ref, vt_ref, e1_ref, cnt_ref, e2_ref, rk2_ref, x1_ref, y_ref,
                    acc_ref, at0_ref, at1_ref, w0_ref, w1_ref):
    e = pl.program_id(1)
    n_tiles = pl.num_programs(1) - 1
    te, tt = at0_ref.shape
    n_i = te // PEER_NKEYS
    ib, jh = 4, PEER_NKEYS // 2

    def activations(u):
        return lax.dot_general(u, h_ref[...], NT_DIMS, preferred_element_type=F32)

    @pl.when(e == 0)
    def _():
        acc_ref[...] = jnp.zeros(acc_ref.shape, F32)
        w1_ref[...] = jnp.zeros(w1_ref.shape, BF)
        at0_ref[...] = activations(u0_ref[...])

    def gate_tile(at_ref, w_ref):
        for lt in range(tt // LANES):
            cols = slice(lt * LANES, (lt + 1) * LANES)
            for i0 in range(0, n_i, ib):
                for j0 in range(0, PEER_NKEYS, jh):
                    g = [jnp.zeros((jh, LANES), F32) for _ in range(ib)]
                    for h in range(PEER_HEADS):
                        rk = rk2_ref[h, j0:j0 + jh, cols]
                        e2 = e2_ref[h, j0:j0 + jh, cols]
                        for i in range(ib):
                            ii = i0 + i
                            sel = rk < cnt_ref[h, ii:ii + 1, cols]
                            g[i] = g[i] + jnp.where(sel, e1_ref[h, ii:ii + 1, cols] * e2, 0.0)
                    for i in range(ib):
                        rows = slice((i0 + i) * PEER_NKEYS + j0, (i0 + i) * PEER_NKEYS + j0 + jh)
                        a = at_ref[rows, cols]
                        gel = 0.5 * a * (1.0 + lax.erf(a * np.float32(2.0 ** -0.5)))
                        w_ref[rows, cols] = (g[i] * gel).astype(BF)

    def phase(at_cur, at_next, w_cur, w_prev):
        at_next[...] = activations(u_ref[...])
        gate_tile(at_cur, w_cur)
        acc_ref[...] += jnp.dot(vt_ref[...], w_prev[...], preferred_element_type=F32)

    parity = lax.rem(e, 2)

    @pl.when(parity == 0)
    def _():
        phase(at0_ref, at1_ref, w0_ref, w1_ref)

    @pl.when(parity == 1)
    def _():
        phase(at1_ref, at0_ref, w1_ref, w0_ref)

    @pl.when(e == n_tiles)
    def _():
        y_ref[...] = x1_ref[...] + acc_ref[...].T


def _table_prep_kernel(u_ref, v_ref, ub_ref, vt_ref):
    ub_ref[...] = u_ref[...].astype(BF)
    vt_ref[...] = v_ref[...].T.astype(BF)


def _cparams(sem):
    return pltpu.CompilerParams(dimension_semantics=sem, vmem_limit_bytes=VMEM_LIMIT)


def _rope_tables(pos):
    inv_freq = ROPE_THETA ** (-jnp.arange(0, QK_ROPE, 2, dtype=F32) / QK_ROPE)
    ang = pos.astype(F32)[:, None] * inv_freq[None, :]
    cos, sin = jnp.cos(ang), jnp.sin(ang)
    n = pos.shape[0]
    zeros = lambda w: jnp.zeros((n, w), F32)
    c = jnp.concatenate([jnp.ones((n, ROPE_LO), F32), cos, cos, zeros(HEAD_PAD - QK_HEAD)], axis=1)
    s1 = jnp.concatenate([zeros(ROPE_LO), -sin, zeros(HEAD_PAD - ROPE_LO - ROPE_HALF)], axis=1)
    s2 = jnp.concatenate([zeros(ROPE_LO + ROPE_HALF), sin, zeros(HEAD_PAD - QK_HEAD)], axis=1)
    return c, s1, s2


def _group_matrix():
    lane = np.arange(256)
    head, within = lane // HEAD_PAD, lane % HEAD_PAD
    group = np.where(within < QK_NOPE, 0, np.where(within < QK_HEAD, 1, 2))
    same = (head[:, None] == head[None, :]) & (group[:, None] == group[None, :]) & (group[:, None] < 2)
    size = np.where(group == 0, QK_NOPE, QK_ROPE)
    return jnp.asarray(same / size[None, :], dtype=BF)


def _pad_heads(w, width):
    lead = w.shape[:-2]
    return jnp.pad(w, [(0, 0)] * (len(lead) + 1) + [(0, HEAD_PAD - width)]).reshape(*lead, HP)


def _prep(w_in, conv_w, w_conv_out, g_q_lat, w_uq, g_q_nope, g_q_rope, g_kv_lat, g_k_rope, w_uk, w_uv,
          g_k_nope, w_attn_out, w_mix_out, norm_mix, norm_ffn, w_peer_q, peer_sub_keys):
    kr_cols = jnp.zeros((D_MODEL, HEAD_PAD), F32).at[:, ROPE_LO:QK_HEAD].set(w_in[:, 2176:2208])
    win = jnp.concatenate([w_in[:, :2176], w_in[:, 2208:], kr_cols], axis=1).astype(BF)
    zpad = jnp.zeros((HEAD_PAD - QK_HEAD,), F32)
    gq = jnp.tile(jnp.concatenate([g_q_nope, g_q_rope, zpad]), N_HEADS)[None] * (ATTN_SCALE * LOG2E)
    gk = jnp.tile(jnp.concatenate([g_k_nope, jnp.zeros((HEAD_PAD - QK_NOPE,), F32)]), N_HEADS)[None]
    gkr = jnp.zeros((1, HEAD_PAD), F32).at[0, ROPE_LO:QK_HEAD].set(g_k_rope)
    wabs = jnp.pad(jnp.transpose(w_uk, (1, 2, 0)), ((0, 0), (0, HEAD_PAD - QK_NOPE), (0, 0))).astype(BF)
    sel = np.zeros((HP, N_HEADS * QK_ROPE), np.float32)
    for h in range(N_HEADS):
        for d in range(QK_ROPE):
            sel[h * HEAD_PAD + ROPE_LO + d, h * QK_ROPE + d] = 1.0
    wao = jnp.pad(w_attn_out.reshape(N_HEADS, V_HEAD, D_MODEL), ((0, 0), (0, HEAD_PAD - V_HEAD), (0, 0)))
    vone = np.zeros((1, HP), np.float32)
    vone[0, ONES_LANE::HEAD_PAD] = 1.0
    return dict(
        win=win, convw=conv_w, wco=w_conv_out.astype(BF), gql=g_q_lat[None], wuq=_pad_heads(w_uq, QK_HEAD).astype(BF),
        gq=gq, ind=_group_matrix(), gkv=g_kv_lat[None], gkr=gkr, wuk=_pad_heads(w_uk, QK_NOPE).astype(BF), gk=gk,
        wuv=_pad_heads(w_uv, V_HEAD).astype(BF), vone=jnp.asarray(vone), wabs=wabs, selr=jnp.asarray(sel, dtype=BF),
        wukt=jnp.transpose(w_uk.reshape(KV_LORA, N_HEADS * QK_NOPE)).astype(BF),
        wao=wao.reshape(HP, D_MODEL).astype(BF), wmo=w_mix_out.astype(BF), nmix=norm_mix[None], nffn=norm_ffn[None],
        wpq=w_peer_q.astype(BF), keys=peer_sub_keys.astype(BF))


def _proj_prompt(x, init, tabs, w, tm):
    nb, n, _ = x.shape
    nt = n // tm
    rows = nb * n
    row_spec = lambda width: pl.BlockSpec((tm, width), lambda b, i: (b * nt + i, 0))
    tab_spec = pl.BlockSpec((tm, HEAD_PAD), lambda b, i: (i, 0))
    weights = [w['nmix'], w['win'], w['convw'], w['wco'], w['gql'], w['wuq'], w['gq'], w['ind'], w['gkv'], w['gkr'],
               w['wuk'], w['gk'], w['wuv'], w['vone']]
    return pl.pallas_call(
        _proj_prompt_kernel,
        grid=(nb, nt),
        in_specs=[pl.BlockSpec((None, tm, D_MODEL), lambda b, i: (b, i, 0)), _const_spec(init.shape),
                  tab_spec, tab_spec, tab_spec] + [_const_spec(a.shape) for a in weights],
        out_specs=[row_spec(D_MODEL), row_spec(D_MODEL), row_spec(HP), row_spec(HP), row_spec(HP),
                   row_spec(KV_LORA), row_spec(HEAD_PAD), pl.BlockSpec((None, 8, CONV_WIDTH), lambda b, i: (b, 0, 0))],
        out_shape=[jax.ShapeDtypeStruct((rows, D_MODEL), F32), jax.ShapeDtypeStruct((rows, D_MODEL), F32),
                   jax.ShapeDtypeStruct((rows, HP), BF), jax.ShapeDtypeStruct((rows, HP), BF),
                   jax.ShapeDtypeStruct((rows, HP), BF), jax.ShapeDtypeStruct((rows, KV_LORA), F32),
                   jax.ShapeDtypeStruct((rows, HEAD_PAD), F32), jax.ShapeDtypeStruct((nb, 8, CONV_WIDTH), F32)],
        scratch_shapes=[pltpu.VMEM((8, CONV_WIDTH), F32)],
        compiler_params=_cparams(("arbitrary", "arbitrary")),
        name="proj_prompt",
    )(x, init, *tabs, *weights)


def _proj_sample(x, prev1, prev2, tabs, w, dec_seq):
    rows = x.shape[0]
    weights = [w['nmix'], w['win'], w['convw'], w['wco'], w['gql'], w['wuq'], w['gq'], w['ind'], w['gkv'], w['gkr'],
               w['gk'], w['wabs'], w['selr']]
    ins = [x, prev1, prev2, *tabs, *weights]
    full = lambda shape: pl.BlockSpec(shape, lambda i: (0,) * len(shape))
    outs = [((rows, D_MODEL), F32), ((rows, D_MODEL), F32), ((rows, N_HEADS * KV_LORA), BF),
            ((rows, N_HEADS * QK_ROPE), BF), ((rows, KV_LORA), F32), ((rows, HEAD_PAD), F32), ((rows, CONV_WIDTH), F32)]
    return pl.pallas_call(
        functools.partial(_proj_sample_kernel, dec_seq=dec_seq),
        grid=(1,),
        in_specs=[full(a.shape) for a in ins],
        out_specs=[full(s) for s, _ in outs],
        out_shape=[jax.ShapeDtypeStruct(s, d) for s, d in outs],
        compiler_params=_cparams(("arbitrary",)),
        name="proj_sample",
    )(*ins)


def _flash(q, k, v, km, vm, nb, n, tq, nh):
    nq = n // tq
    w = nh * HEAD_PAD
    return pl.pallas_call(
        functools.partial(_flash_kernel, tq=tq, nh=nh),
        grid=(nb, N_HEADS // nh, nq),
        in_specs=[pl.BlockSpec((tq, w), lambda b, h, i: (b * nq + i, h)),
                  pl.BlockSpec((n, w), lambda b, h, i: (b, h)),
                  pl.BlockSpec((n, w), lambda b, h, i: (b, h)),
                  pl.BlockSpec((HEAD_PAD, w), lambda b, h, i: (0, h)),
                  pl.BlockSpec((HEAD_PAD, w), lambda b, h, i: (0, h))],
        out_specs=pl.BlockSpec((tq, w), lambda b, h, i: (b * nq + i, h)),
        out_shape=jax.ShapeDtypeStruct((nb * n, HP), BF),
        scratch_shapes=[pltpu.VMEM((nh, tq, HEAD_PAD), F32), pltpu.VMEM((nh, tq, HEAD_PAD), F32)],
        compiler_params=_cparams(("arbitrary", "arbitrary", "arbitrary")),
        name="flash",
    )(q, k, v, km, vm)


def _sample_attn(page_table, qabs, qr, cnew, krnew, wukt, pool_ckv, pool_kr, n_pages):
    nseq, n_q, _ = qabs.shape
    dec_seq = cnew.shape[1]
    page = pool_ckv.shape[1]
    steps = page_table.shape[1] // n_pages
    seq_spec = lambda a: pl.BlockSpec((None,) + a.shape[1:], lambda b, p, pt: (b, 0, 0))
    page_spec = lambda a, j: pl.BlockSpec((None,) + a.shape[1:], lambda b, p, pt: (pt[b, p * n_pages + j], 0, 0))
    return pl.pallas_call(
        functools.partial(_sample_attn_kernel, n_pages=n_pages, dec_seq=dec_seq),
        grid_spec=pltpu.PrefetchScalarGridSpec(
            num_scalar_prefetch=1,
            grid=(nseq, steps),
            in_specs=[seq_spec(qabs), seq_spec(qr), seq_spec(cnew), seq_spec(krnew),
                      pl.BlockSpec(wukt.shape, lambda b, p, pt: (0, 0))]
                     + [page_spec(pool_ckv, j) for j in range(n_pages)]
                     + [page_spec(pool_kr, j) for j in range(n_pages)],
            out_specs=pl.BlockSpec((None, n_q, KV_LORA), lambda b, p, pt: (b, 0, 0)),
            scratch_shapes=[pltpu.VMEM((N_HEADS * QK_NOPE + n_q, KV_LORA), BF), pltpu.VMEM((n_q, 1), F32),
                            pltpu.VMEM((n_q, 1), F32), pltpu.VMEM((n_q, KV_LORA), F32),
                            pltpu.VMEM((page, KV_LORA), F32), pltpu.VMEM((page, QK_ROPE), F32)]),
        out_shape=jax.ShapeDtypeStruct((nseq, n_q, KV_LORA), F32),
        compiler_params=_cparams(("arbitrary", "arbitrary")),
        name="sample_attn",
    )(page_table, qabs, qr, cnew, krnew, wukt, *([pool_ckv] * n_pages), *([pool_kr] * n_pages))


def _merge(x, attn, gc, sa, w, tm, latent):
    rows = x.shape[0]
    row_spec = lambda width: pl.BlockSpec((tm, width), lambda i: (i, 0))
    weights = [w['wuv'], w['wao'], w['wmo'], w['nffn'], w['wpq'], w['keys']]
    return pl.pallas_call(
        functools.partial(_merge_kernel, latent=latent),
        grid=(rows // tm,),
        in_specs=[row_spec(D_MODEL), row_spec(attn.shape[1]), row_spec(D_MODEL), row_spec(D_MODEL)]
                 + [_const_spec(a.shape) for a in weights],
        out_specs=[row_spec(D_MODEL), row_spec(D_MODEL),
                   pl.BlockSpec((2 * PEER_HEADS, PEER_NKEYS, tm), lambda i: (0, 0, i))],
        out_shape=[jax.ShapeDtypeStruct((rows, D_MODEL), F32), jax.ShapeDtypeStruct((rows, D_MODEL), BF),
                   jax.ShapeDtypeStruct((2 * PEER_HEADS, PEER_NKEYS, rows), F32)],
        compiler_params=_cparams(("arbitrary",)),
        name="merge",
    )(x, attn, gc, sa, *weights)


def _select(st):
    rows = st.shape[2]
    head_spec = pl.BlockSpec((PEER_HEADS, PEER_NKEYS, LANES), lambda i: (0, 0, i))
    out = jax.ShapeDtypeStruct((PEER_HEADS, PEER_NKEYS, rows), F32)
    return pl.pallas_call(
        _select_kernel,
        grid=(rows // LANES,),
        in_specs=[pl.BlockSpec((2 * PEER_HEADS, PEER_NKEYS, LANES), lambda i: (0, 0, i))],
        out_specs=[head_spec] * 4,
        out_shape=[out] * 4,
        scratch_shapes=[pltpu.VMEM((2 * PEER_HEADS, PEER_NKEYS, LANES), F32),
                        pltpu.VMEM((2 * PEER_HEADS, PEER_TOPK, LANES), F32)],
        compiler_params=_cparams(("arbitrary",)),
        name="select",
    )(st)


def _experts(h2, x1, e1, cnt, e2, rk2, u_b, vt_b, tt, te):
    rows = h2.shape[0]
    n_i = te // PEER_NKEYS
    n_tiles = PEER_EXPERTS // te
    last = n_tiles - 1
    tok_spec = pl.BlockSpec((tt, D_MODEL), lambda t, e: (t, 0))
    i_spec = pl.BlockSpec((PEER_HEADS, n_i, tt), lambda t, e: (0, jnp.minimum(e, last), t))
    j_spec = pl.BlockSpec((PEER_HEADS, PEER_NKEYS, tt), lambda t, e: (0, 0, t))
    return pl.pallas_call(
        _experts_kernel,
        grid=(rows // tt, n_tiles + 1),
        in_specs=[tok_spec, pl.BlockSpec((te, D_MODEL), lambda t, e: (0, 0)),
                  pl.BlockSpec((te, D_MODEL), lambda t, e: (jnp.minimum(e + 1, last), 0)),
                  pl.BlockSpec((D_MODEL, te), lambda t, e: (0, jnp.clip(e - 1, 0, last))),
                  i_spec, i_spec, j_spec, j_spec, tok_spec],
        out_specs=tok_spec,
        out_shape=jax.ShapeDtypeStruct((rows, D_MODEL), F32),
        scratch_shapes=[pltpu.VMEM((D_MODEL, tt), F32), pltpu.VMEM((te, tt), F32), pltpu.VMEM((te, tt), F32),
                        pltpu.VMEM((te, tt), BF), pltpu.VMEM((te, tt), BF)],
        compiler_params=_cparams(("arbitrary", "arbitrary")),
        name="experts",
    )(h2, u_b, u_b, vt_b, e1, cnt, e2, rk2, x1)


def _table_prep(peer_u, peer_v, tr):
    n, d = peer_u.shape
    return pl.pallas_call(
        _table_prep_kernel,
        grid=(n // tr,),
        in_specs=[pl.BlockSpec((tr, d), lambda i: (i, 0)), pl.BlockSpec((tr, d), lambda i: (i, 0))],
        out_specs=[pl.BlockSpec((tr, d), lambda i: (i, 0)), pl.BlockSpec((d, tr), lambda i: (0, i))],
        out_shape=[jax.ShapeDtypeStruct((n, d), BF), jax.ShapeDtypeStruct((d, n), BF)],
        compiler_params=_cparams(("arbitrary",)),
        name="table_prep",
    )(peer_u, peer_v)


def _peer(x, attn, gc, sa, w, u_b, vt_b, tm, tt, te, latent):
    x1, h2, st = _merge(x, attn, gc, sa, w, tm, latent)
    e1, cnt, e2, rk2 = _select(st)
    return _experts(h2, x1, e1, cnt, e2, rk2, u_b, vt_b, tt, te)


def kernel(x_prompt, x_sample, cache_ckv, cache_krope, state_conv, page_table, meta_tokens, norm_mix, w_in, conv_w, w_conv_out, g_q_lat, w_uq, g_q_nope, g_q_rope, g_kv_lat, g_k_rope, w_uk, w_uv, g_k_nope, w_attn_out, w_mix_out, norm_ffn, w_peer_q, peer_sub_keys, peer_u, peer_v):
    depth = w_in.shape[0]
    assert depth == 1, "single-layer step"
    nb, n_real, _ = x_prompt.shape
    nseq, dec_seq, _ = x_sample.shape
    past_len = page_table.shape[1] * cache_ckv.shape[2]
    assert dec_seq >= CONV_K - 1

    w = _prep(w_in[0], conv_w[0], w_conv_out[0], g_q_lat[0], w_uq[0], g_q_nope[0], g_q_rope[0], g_kv_lat[0],
              g_k_rope[0], w_uk[0], w_uv[0], g_k_nope[0], w_attn_out[0], w_mix_out[0], norm_mix[0], norm_ffn[0],
              w_peer_q[0], peer_sub_keys[0])
    u_b, vt_b = _table_prep(peer_u[0], peer_v[0], 512)

    tabs_p = _rope_tables(jnp.arange(N_META + n_real))
    tabs_meta = tuple(t[:N_META] for t in tabs_p)
    tabs_real = tuple(t[N_META:] for t in tabs_p)
    tabs_s = tuple(jnp.tile(t, (nseq, 1)) for t in _rope_tables(past_len + jnp.arange(dec_seq)))

    zero_init = jnp.zeros((8, CONV_WIDTH), F32)
    _, _, _, k_m, v_m, ckv_m, kr_m, u_m = _proj_prompt(meta_tokens[None], zero_init, tabs_meta, w, N_META)
    gc, sa, q, k, v, ckv, kr, utail = _proj_prompt(x_prompt, u_m[0], tabs_real, w, 256)
    pad_rows = lambda a: jnp.pad(a, ((0, HEAD_PAD - N_META), (0, 0)))
    attn = _flash(q, k, v, pad_rows(k_m), pad_rows(v_m), nb, n_real, 512, 2)
    y_prompt = _peer(x_prompt.reshape(nb * n_real, D_MODEL), attn, gc, sa, w, u_b, vt_b, 256, 512, 1024, False)

    st = state_conv[0]
    prev2 = jnp.pad(st, ((0, 0), (0, dec_seq - (CONV_K - 1)), (0, 0))).reshape(nseq * dec_seq, CONV_WIDTH)
    prev1 = jnp.pad(st[:, 1:], ((0, 0), (0, dec_seq - 1), (0, 0))).reshape(nseq * dec_seq, CONV_WIDTH)
    xs = x_sample.reshape(nseq * dec_seq, D_MODEL)
    gc_s, sa_s, qabs, qr, ckv_s, kr_s, u_s = _proj_sample(xs, prev1, prev2, tabs_s, w, dec_seq)
    o_lat = _sample_attn(
        page_table, qabs.reshape(nseq, dec_seq * N_HEADS, KV_LORA), qr.reshape(nseq, dec_seq * N_HEADS, QK_ROPE),
        ckv_s.reshape(nseq, dec_seq, KV_LORA), kr_s[:, ROPE_LO:QK_HEAD].reshape(nseq, dec_seq, QK_ROPE),
        w['wukt'], cache_ckv[0], cache_krope[0], 16)
    y_sample = _peer(xs, o_lat.reshape(nseq * dec_seq, N_HEADS * KV_LORA), gc_s, sa_s, w, u_b, vt_b,
                     256, 512, 1024, True)

    bcast = lambda a: jnp.broadcast_to(a[None], (nb,) + a.shape)
    new_ckv_p = jnp.concatenate([bcast(ckv_m), ckv.reshape(nb, n_real, KV_LORA)], axis=1)
    kr_all = jnp.concatenate([bcast(kr_m), kr.reshape(nb, n_real, HEAD_PAD)], axis=1)[:, :, ROPE_LO:QK_HEAD]
    return (y_prompt.reshape(nb, n_real, D_MODEL), y_sample.reshape(nseq, dec_seq, D_MODEL),
            new_ckv_p[None], kr_all[None], utail[:, 8 - (CONV_K - 1):][None],
            ckv_s.reshape(1, nseq, dec_seq, KV_LORA),
            kr_s[:, ROPE_LO:QK_HEAD].reshape(1, nseq, dec_seq, QK_ROPE),
            u_s.reshape(nseq, dec_seq, CONV_WIDTH)[:, dec_seq - (CONV_K - 1):][None])
```
